```python
import jax, jax.numpy as jnp
from jax import lax
import numpy as np

D_MODEL = 2048
BATCH = 4
SEQ = 2048
DEPTH = 1
DEC_BATCH = 128
DEC_SEQ = 1
PAST_LEN = 16384
PAGE_SIZE = 128

CHUNK = 128
A_WIDTH = D_MODEL // 2
A_GROUPS = 4
A_GROUP_W = A_WIDTH // A_GROUPS
R_HEADS = 8
R_WIDTH = D_MODEL // 2
R_HEAD_DIM = R_WIDTH // R_HEADS
RET_CHUNK = 128
ROPE_BASE = 10000.0
M_HEADS = 4
M_HEAD_DIM = 128
M_WIDTH = M_HEADS * M_HEAD_DIM
N_MEM = 256
N_BRANCH = 3
N_EXPERTS = 32
TOP_K = 4
D_FF = D_MODEL
SWIGLU_ALPHA = 1.702
SWIGLU_LIMIT = 7.0
MOE_BLOCK = 128
EPS = 1e-6
IN_WIDTH = 2 * A_WIDTH + 4 * R_WIDTH + M_WIDTH + N_BRANCH * D_MODEL

kernel_name = "hybrid_gmlp_retention_memory_moe_step"

F32 = jnp.float32


def rmsnorm(x, g):
    xf = x.astype(F32)
    xf = xf * lax.rsqrt(jnp.mean(xf * xf, axis=-1, keepdims=True) + EPS)
    return (xf * g.astype(F32)).astype(x.dtype)


def layernorm(x, g):
    xf = x.astype(F32)
    xc = xf - jnp.mean(xf, axis=-1, keepdims=True)
    xf = xc * lax.rsqrt(jnp.mean(xc * xc, axis=-1, keepdims=True) + EPS)
    return (xf * g.astype(F32)).astype(x.dtype)


def rotary(x, pos):
    half = x.shape[-1] // 2
    inv = ROPE_BASE ** (-jnp.arange(half, dtype=F32) / half)
    ang = pos[:, None] * inv[None, :]
    cos = jnp.cos(ang)[None, :, None, :].astype(x.dtype)
    sin = jnp.sin(ang)[None, :, None, :].astype(x.dtype)
    x1, x2 = x[..., :half], x[..., half:]
    return jnp.concatenate([x1 * cos - x2 * sin, x1 * sin + x2 * cos], axis=-1)


def chunk_spatial_gate(u, v, w_s, b_s):
    B, T, _ = v.shape
    n_chunks = -(-T // CHUNK)
    pad = n_chunks * CHUNK - T
    vp = jnp.pad(v, ((0, 0), (0, pad), (0, 0))).reshape(B, n_chunks, CHUNK, A_GROUPS, A_GROUP_W)
    causal = jnp.tril(jnp.ones((CHUNK, CHUNK), dtype=bool))
    w = jnp.where(causal[None], w_s, 0.0).astype(v.dtype)
    s = jnp.einsum('gij,bnjgc->bnigc', w, vp) + b_s.T[:, :, None].astype(v.dtype)
    s = s.reshape(B, n_chunks * CHUNK, A_WIDTH)[:, :T]
    return u * s


def retention(q, k, v, state0, log_gamma):
    B, T, H, dk = q.shape
    dv = v.shape[-1]
    c = RET_CHUNK if T % RET_CHUNK == 0 else T
    n = T // c

    def to_chunks(a):
        return a.astype(F32).reshape(B, n, c, H, a.shape[-1]).transpose(1, 0, 2, 3, 4)

    qc, kc, vc = to_chunks(q), to_chunks(k), to_chunks(v)
    idx = jnp.arange(c, dtype=F32)
    diff = idx[:, None] - idx[None, :]
    decay = jnp.where(diff[None] >= 0,
                      jnp.exp(jnp.maximum(diff, 0.0)[None] * log_gamma[:, None, None]), 0.0)
    q_decay = jnp.exp((idx[:, None] + 1.0) * log_gamma[None, :])
    k_decay = jnp.exp((c - 1.0 - idx[:, None]) * log_gamma[None, :])
    chunk_decay = jnp.exp(c * log_gamma)

    def step(S, blk):
        qb, kb, vb = blk
        scores = jnp.einsum('bihd,bjhd->bhij', qb, kb) * decay[None]
        o = (jnp.einsum('bhij,bjhe->bihe', scores, vb)
             + jnp.einsum('bihd,bhde->bihe', qb * q_decay[None, :, :, None], S))
        S = (S * chunk_decay[None, :, None, None]
             + jnp.einsum('bjhd,bjhe->bhde', kb * k_decay[None, :, :, None], vb))
        return S, o

    S, o = lax.scan(step, state0.astype(F32), (qc, kc, vc))
    o = o.transpose(1, 0, 2, 3, 4).reshape(B, T, H, dv)
    return o, S


def memory_kv(mem, g_mem, w_mk, w_mv):
    m = rmsnorm(mem, g_mem)
    B = mem.shape[0]
    k = (m @ w_mk).reshape(B, N_MEM, M_HEADS, M_HEAD_DIM)
    v = (m @ w_mv).reshape(B, N_MEM, M_HEADS, M_HEAD_DIM)
    return k, v


def memory_attend(q, k, v):
    scores = jnp.einsum('bthd,bmhd->bhtm', q.astype(F32), k.astype(F32)) * (M_HEAD_DIM ** -0.5)
    p = jax.nn.softmax(scores, axis=-1)
    return jnp.einsum('bhtm,bmhd->bthd', p, v.astype(F32)).astype(q.dtype)


def mixer_block(h, pos, mem_k, mem_v, ret_state0, log_gamma,
                w_in, ln_v_g, w_s, b_s, ret_norm_g, w_a, w_b, w_m, w_o):
    B, T, _ = h.shape
    p = h @ w_in
    bounds = [int(s) for s in np.cumsum([A_WIDTH, A_WIDTH, R_WIDTH, R_WIDTH, R_WIDTH,
                                         R_WIDTH, M_WIDTH, D_MODEL, D_MODEL])]
    pu, pv, pq, pk, pvr, pgr, pqm, pga, pgb, pgm = jnp.split(p, bounds, axis=-1)

    u = jax.nn.gelu(pu)
    v = layernorm(jax.nn.gelu(pv), ln_v_g)
    a_out = chunk_spatial_gate(u, v, w_s, b_s)
    v_rows = v[:, ((T - 1) // CHUNK) * CHUNK:]

    q = rotary(pq.reshape(B, T, R_HEADS, R_HEAD_DIM), pos) * (R_HEAD_DIM ** -0.5)
    k = rotary(pk.reshape(B, T, R_HEADS, R_HEAD_DIM), pos)
    vr = pvr.reshape(B, T, R_HEADS, R_HEAD_DIM)
    o, S = retention(q, k, vr, ret_state0, log_gamma)
    o = layernorm(o, ret_norm_g.reshape(R_HEADS, R_HEAD_DIM)).reshape(B, T, R_WIDTH)
    b_out = (o * jax.nn.silu(pgr.astype(F32))).astype(h.dtype)

    m_out = memory_attend(pqm.reshape(B, T, M_HEADS, M_HEAD_DIM), mem_k, mem_v).reshape(B, T, M_WIDTH)

    merged = (jax.nn.sigmoid(pga) * (a_out @ w_a)
              + jax.nn.sigmoid(pgb) * (b_out @ w_b)
              + jax.nn.sigmoid(pgm) * (m_out @ w_m))
    return merged @ w_o, S.astype(ret_state0.dtype), v_rows


def moe(h, w_router, b_router, w1, b1, w2, b2):
    lead = h.shape[:-1]
    x = h.reshape(-1, D_MODEL)
    T = x.shape[0]
    logits = x.astype(F32) @ w_router.astype(F32) + b_router.astype(F32)
    top_val, top_idx = lax.top_k(logits, TOP_K)
    top_w = jax.nn.softmax(top_val, axis=-1).astype(x.dtype)

    A = T * TOP_K
    flat_e = top_idx.reshape(A).astype(jnp.int32)
    flat_tok = jnp.arange(A, dtype=jnp.int32) // TOP_K
    order = jnp.argsort(flat_e)
    sorted_e = flat_e[order]
    sorted_tok = flat_tok[order]
    sorted_w = top_w.reshape(A)[order]
    counts = jnp.bincount(flat_e, length=N_EXPERTS).astype(jnp.int32)
    padded = ((counts + MOE_BLOCK - 1) // MOE_BLOCK) * MOE_BLOCK
    pad_end = jnp.cumsum(padded)
    pad_start = pad_end - padded
    start = jnp.cumsum(counts) - counts
    dest = pad_start[sorted_e] + jnp.arange(A, dtype=jnp.int32) - start[sorted_e]
    n_blocks = -(-A // MOE_BLOCK) + N_EXPERTS
    P = n_blocks * MOE_BLOCK
    buf_tok = jnp.zeros((P,), jnp.int32).at[dest].set(sorted_tok)
    block_start = jnp.arange(n_blocks, dtype=jnp.int32) * MOE_BLOCK
    block_e = jnp.minimum(jnp.searchsorted(pad_end, block_start, side='right'), N_EXPERTS - 1)
    xs = x[buf_tok].reshape(n_blocks, MOE_BLOCK, D_MODEL)

    def expert_block(args):
        xb, e = args
        hh = xb @ w1[e] + b1[e]
        glu = jnp.minimum(hh[:, ::2], SWIGLU_LIMIT)
        lin = jnp.clip(hh[:, 1::2], -SWIGLU_LIMIT, SWIGLU_LIMIT)
        act = glu * jax.nn.sigmoid(SWIGLU_ALPHA * glu) * (lin + 1.0)
        return act @ w2[e] + b2[e]

    ys = lax.map(expert_block, (xs, block_e)).reshape(P, D_MODEL)
    out = jnp.zeros_like(x).at[sorted_tok].add(ys[dest] * sorted_w[:, None])
    return out.reshape(*lead, D_MODEL)


def decoder_layer(x, pos, mem_k, mem_v, ret_state0, log_gamma,
                  g_mix, w_in, ln_v_g, w_s, b_s, ret_norm_g, w_a, w_b, w_m, w_o,
                  g_ffn, w_router, b_router, w1, b1, w2, b2):
    mix, S, v_rows = mixer_block(rmsnorm(x, g_mix), pos, mem_k, mem_v, ret_state0, log_gamma,
                                 w_in, ln_v_g, w_s, b_s, ret_norm_g, w_a, w_b, w_m, w_o)
    x = x + mix
    x = x + moe(rmsnorm(x, g_ffn), w_router, b_router, w1, b1, w2, b2)
    return x, S, v_rows


def setup_inputs(seed: int = 0) -> dict:
    key = jax.random.key(seed)
    ks = jax.random.split(key, 32)

    def nrm(k, shape, scale):
        return jax.random.normal(k, shape, F32) * scale

    def gain(k, shape):
        return 1.0 + 0.02 * jax.random.normal(k, shape, F32)

    L = DEPTH
    return {
        "x_prompt": nrm(ks[0], (BATCH, SEQ, D_MODEL), 1.0),
        "x_sample": nrm(ks[1], (DEC_BATCH, DEC_SEQ, D_MODEL), 1.0),
        "cache_mem_k": nrm(ks[2], (L, DEC_BATCH, N_MEM, M_HEADS, M_HEAD_DIM), 1.0),
        "cache_mem_v": nrm(ks[3], (L, DEC_BATCH, N_MEM, M_HEADS, M_HEAD_DIM), 1.0),
        "state_ret": nrm(ks[4], (L, DEC_BATCH, R_HEADS, R_HEAD_DIM, R_HEAD_DIM), 1.0),
        "mem_prompt": nrm(ks[5], (BATCH, N_MEM, D_MODEL), 1.0),
        "g_mix": gain(ks[6], (L, D_MODEL)),
        "w_in": nrm(ks[7], (L, D_MODEL, IN_WIDTH), D_MODEL ** -0.5),
        "ln_v_g": gain(ks[8], (L, A_WIDTH)),
        "w_s": nrm(ks[9], (L, A_GROUPS, CHUNK, CHUNK), CHUNK ** -0.5),
        "b_s": gain(ks[10], (L, A_GROUPS, CHUNK)),
        "ret_norm_g": gain(ks[11], (L, R_WIDTH)),
        "g_mem": gain(ks[12], (L, D_MODEL)),
        "w_mk": nrm(ks[13], (L, D_MODEL, M_WIDTH), D_MODEL ** -0.5),
        "w_mv": nrm(ks[14], (L, D_MODEL, M_WIDTH), D_MODEL ** -0.5),
        "w_a": nrm(ks[15], (L, A_WIDTH, D_MODEL), A_WIDTH ** -0.5),
        "w_b": nrm(ks[16], (L, R_WIDTH, D_MODEL), R_WIDTH ** -0.5),
        "w_m": nrm(ks[17], (L, M_WIDTH, D_MODEL), M_WIDTH ** -0.5),
        "w_o": nrm(ks[18], (L, D_MODEL, D_MODEL), D_MODEL ** -0.5),
        "g_ffn": gain(ks[19], (L, D_MODEL)),
        "w_router": nrm(ks[20], (L, D_MODEL, N_EXPERTS), D_MODEL ** -0.5),
        "b_router": nrm(ks[21], (L, N_EXPERTS), 0.01),
        "w1": nrm(ks[22], (L, N_EXPERTS, D_MODEL, 2 * D_FF), D_MODEL ** -0.5),
        "b1": nrm(ks[23], (L, N_EXPERTS, 2 * D_FF), 0.01),
        "w2": nrm(ks[24], (L, N_EXPERTS, D_FF, D_MODEL), D_FF ** -0.5),
        "b2": nrm(ks[25], (L, N_EXPERTS, D_MODEL), 0.01),
        "g_final": gain(ks[26], (D_MODEL,)),
    }


def reference(x_prompt, x_sample, cache_mem_k, cache_mem_v, state_ret, mem_prompt,
              g_mix, w_in, ln_v_g, w_s, b_s, ret_norm_g, g_mem, w_mk, w_mv,
              w_a, w_b, w_m, w_o, g_ffn, w_router, b_router, w1, b1, w2, b2, g_final):
    pos_p = jnp.arange(x_prompt.shape[1], dtype=F32)
    pos_s = PAST_LEN + jnp.arange(x_sample.shape[1], dtype=F32)
    log_gamma = jnp.log(1.0 - 2.0 ** (-5.0 - jnp.arange(R_HEADS, dtype=F32)))
    n_prompt = x_prompt.shape[0]

    hp, hs = x_prompt, x_sample
    mk_list, mv_list, sp_list, vp_list, ss_list, vs_list = [], [], [], [], [], []
    for l in range(DEPTH):
        mk, mv = memory_kv(mem_prompt, g_mem[l], w_mk[l], w_mv[l])
        ret0 = jnp.zeros((n_prompt, R_HEADS, R_HEAD_DIM, R_HEAD_DIM), state_ret.dtype)
        lw = (g_mix[l], w_in[l], ln_v_g[l], w_s[l], b_s[l], ret_norm_g[l], w_a[l], w_b[l], w_m[l], w_o[l],
              g_ffn[l], w_router[l], b_router[l], w1[l], b1[l], w2[l], b2[l])
        hp, sp, vp = decoder_layer(hp, pos_p, mk, mv, ret0, log_gamma, *lw)
        hs, ss, vs = decoder_layer(hs, pos_s, cache_mem_k[l], cache_mem_v[l], state_ret[l], log_gamma, *lw)
        mk_list.append(mk)
        mv_list.append(mv)
        sp_list.append(sp)
        vp_list.append(vp)
        ss_list.append(ss)
        vs_list.append(vs)

    y_prompt = rmsnorm(hp, g_final)
    y_sample = rmsnorm(hs, g_final)
    mem_k_prompt = jnp.stack(mk_list)
    mem_v_prompt = jnp.stack(mv_list)
    ret_state_prompt = jnp.stack(sp_list)
    chunk_v_prompt = jnp.stack(vp_list)
    ret_state_sample = jnp.stack(ss_list)
    chunk_v_sample = jnp.stack(vs_list)
    return (y_prompt, y_sample, mem_k_prompt, mem_v_prompt, ret_state_prompt, chunk_v_prompt,
            ret_state_sample, chunk_v_sample)
```

```python
import functools

import jax
import jax.numpy as jnp
import numpy as np
from jax import lax
from jax.experimental import pallas as pl
from jax.experimental.pallas import tpu as pltpu

F32 = jnp.float32
BF16 = jnp.bfloat16

D_MODEL = 2048
CHUNK = 128
A_WIDTH = 1024
A_GROUPS = 4
A_GROUP_W = A_WIDTH // A_GROUPS
R_HEADS = 8
R_WIDTH = 1024
R_HEAD_DIM = 128
ROPE_BASE = 10000.0
M_HEADS = 4
M_HEAD_DIM = 128
M_WIDTH = 512
N_MEM = 256
N_EXPERTS = 32
TOP_K = 4
D_FF = 2048
SWIGLU_ALPHA = 1.702
SWIGLU_LIMIT = 7.0
EPS = 1e-6
PAST_LEN = 16384

MAIN_W = 2 * A_WIDTH + 4 * R_WIDTH
GATE_W = 3 * D_MODEL
IN_WIDTH = MAIN_W + M_WIDTH + GATE_W

V7X_VMEM_BYTES = 64 * 1024 * 1024
VMEM_LIMIT = V7X_VMEM_BYTES - 8 * 1024 * 1024

IN_TN = 512
IN_TM = 1040
MERGE_TM = 320
MOE_TM = 512
MOE_TF = 512
SAMPLE_BB = 8


def _params(sem):
    return pltpu.CompilerParams(dimension_semantics=sem, vmem_limit_bytes=VMEM_LIMIT)


def _rms(x, g):
    return x * lax.rsqrt(jnp.mean(x * x, axis=-1, keepdims=True) + EPS) * g


def _ln(x, g):
    xc = x - jnp.mean(x, axis=-1, keepdims=True)
    return xc * lax.rsqrt(jnp.mean(xc * xc, axis=-1, keepdims=True) + EPS) * g


def _memkv_kernel(m_ref, g_ref, wk_ref, wv_ref, k_ref, v_ref):
    m = _rms(m_ref[...], g_ref[...]).astype(BF16)
    k_ref[...] = jnp.dot(m, wk_ref[...], preferred_element_type=F32)
    v_ref[...] = jnp.dot(m, wv_ref[...], preferred_element_type=F32)


def _memkv(mem, g_mem, wk, wv):
    rows = mem.shape[0]
    tm = 256
    return pl.pallas_call(
        _memkv_kernel,
        grid=(rows // tm,),
        in_specs=[
            pl.BlockSpec((tm, D_MODEL), lambda i: (i, 0)),
            pl.BlockSpec((1, D_MODEL), lambda i: (0, 0)),
            pl.BlockSpec((D_MODEL, M_WIDTH), lambda i: (0, 0)),
            pl.BlockSpec((D_MODEL, M_WIDTH), lambda i: (0, 0)),
        ],
        out_specs=[
            pl.BlockSpec((tm, M_WIDTH), lambda i: (i, 0)),
            pl.BlockSpec((tm, M_WIDTH), lambda i: (i, 0)),
        ],
        out_shape=[jax.ShapeDtypeStruct((rows, M_WIDTH), F32)] * 2,
        compiler_params=_params(("arbitrary",)),
        name="memkv",
    )(mem, g_mem, wk, wv)


_N_MAIN_T = MAIN_W // IN_TN
_N_GATE_T = GATE_W // IN_TN
_N_IN_T = IN_WIDTH // IN_TN
_QM_SRC_T = MAIN_W // IN_TN


def _in_src_tile(j):
    return jnp.where(j < _N_MAIN_T, j, jnp.where(j < _N_MAIN_T + _N_GATE_T, j + 1, _QM_SRC_T))


def _inproj_kernel(x_ref, g_ref, w_ref, o_ref, h_scr):
    j = pl.program_id(1)

    @pl.when(j == 0)
    def _():
        h_scr[...] = _rms(x_ref[...], g_ref[...]).astype(BF16)

    acc = jnp.dot(h_scr[...], w_ref[...], preferred_element_type=F32)
    n_gelu = 2 * A_WIDTH // IN_TN
    silu_lo = (2 * A_WIDTH + 3 * R_WIDTH) // IN_TN

    @pl.when(j < n_gelu)
    def _():
        o_ref[...] = jax.nn.gelu(acc).astype(o_ref.dtype)

    @pl.when(jnp.logical_or(jnp.logical_and(j >= n_gelu, j < silu_lo), j == _N_IN_T - 1))
    def _():
        o_ref[...] = acc.astype(o_ref.dtype)

    @pl.when(jnp.logical_and(j >= silu_lo, j < _N_MAIN_T))
    def _():
        o_ref[...] = (acc * jax.nn.sigmoid(acc)).astype(o_ref.dtype)

    @pl.when(jnp.logical_and(j >= _N_MAIN_T, j < _N_IN_T - 1))
    def _():
        o_ref[...] = jax.nn.sigmoid(acc).astype(o_ref.dtype)


def _inproj(x, g_mix, w_in):
    rows = x.shape[0]
    return pl.pallas_call(
        _inproj_kernel,
        grid=(rows // IN_TM, _N_IN_T),
        in_specs=[
            pl.BlockSpec((IN_TM, D_MODEL), lambda i, j: (i, 0)),
            pl.BlockSpec((1, D_MODEL), lambda i, j: (0, 0)),
            pl.BlockSpec((D_MODEL, IN_TN), lambda i, j: (0, _in_src_tile(j))),
        ],
        out_specs=pl.BlockSpec((IN_TM, IN_TN), lambda i, j: (i, j)),
        out_shape=jax.ShapeDtypeStruct((rows, IN_WIDTH), BF16),
        scratch_shapes=[pltpu.VMEM((IN_TM, D_MODEL), BF16)],
        compiler_params=_params(("arbitrary", "arbitrary")),
        name="inproj",
    )(x, g_mix, w_in)


def _rotary(x, cos, sin_signed):
    return x * cos + pltpu.roll(x, R_HEAD_DIM // 2, axis=1) * sin_signed


def _mixer_prompt_kernel(cd_ref, p_ref, qm_ref, cos_ref, sin_ref, mk_ref, mv_ref, ws_ref, bs_ref,
                         lnv_ref, rng_ref, decay_ref, qd_ref, kd_ref,
                         a_ref, b_ref, m_ref, vrows_ref, state_ref, s_scr):
    c = pl.program_id(1)

    @pl.when(c == 0)
    def _():
        s_scr[...] = jnp.zeros_like(s_scr)

    v = _ln(p_ref[:, A_WIDTH:2 * A_WIDTH].astype(F32), lnv_ref[...])
    vrows_ref[0, 0] = v
    vb = v.astype(BF16)
    row = lax.broadcasted_iota(jnp.int32, (CHUNK, CHUNK), 0)
    col = lax.broadcasted_iota(jnp.int32, (CHUNK, CHUNK), 1)
    causal = row >= col
    for g in range(A_GROUPS):
        lo, hi = g * A_GROUP_W, (g + 1) * A_GROUP_W
        w = jnp.where(causal, ws_ref[g], 0.0).astype(BF16)
        s = jnp.dot(w, vb[:, lo:hi], preferred_element_type=F32) + bs_ref[:, g:g + 1]
        a_ref[:, lo:hi] = (p_ref[:, lo:hi].astype(F32) * s).astype(a_ref.dtype)

    cos = cos_ref[...]
    sin = sin_ref[...]
    q_off, k_off, v_off, g_off = 2 * A_WIDTH, 2 * A_WIDTH + R_WIDTH, 2 * A_WIDTH + 2 * R_WIDTH, 2 * A_WIDTH + 3 * R_WIDTH
    nt = (((1,), (1,)), ((), ()))
    tn = (((0,), (0,)), ((), ()))
    for h in range(R_HEADS):
        lo, hi = h * R_HEAD_DIM, (h + 1) * R_HEAD_DIM
        q = _rotary(p_ref[:, q_off + lo:q_off + hi].astype(F32), cos, sin) * (R_HEAD_DIM ** -0.5)
        k = _rotary(p_ref[:, k_off + lo:k_off + hi].astype(F32), cos, sin)
        vr = p_ref[:, v_off + lo:v_off + hi]
        s0 = s_scr[h]
        scores = lax.dot_general(q.astype(BF16), k.astype(BF16), nt, preferred_element_type=F32) * decay_ref[h]
        o = (jnp.dot(scores.astype(BF16), vr, preferred_element_type=F32)
             + jnp.dot((q * qd_ref[:, h:h + 1]).astype(BF16), s0.astype(BF16), preferred_element_type=F32))
        s_scr[h] = s0 * cd_ref[h] + lax.dot_general((k * kd_ref[:, h:h + 1]).astype(BF16), vr, tn,
                                                     preferred_element_type=F32)
        o = _ln(o, rng_ref[:, lo:hi]) * p_ref[:, g_off + lo:g_off + hi].astype(F32)
        b_ref[:, lo:hi] = o.astype(b_ref.dtype)

    @pl.when(c == pl.num_programs(1) - 1)
    def _():
        state_ref[0, 0] = s_scr[...]

    for h in range(M_HEADS):
        lo, hi = h * M_HEAD_DIM, (h + 1) * M_HEAD_DIM
        kh = mk_ref[0, :, lo:hi].astype(BF16)
        vh = mv_ref[0, :, lo:hi].astype(BF16)
        sc = lax.dot_general(qm_ref[:, lo:hi], kh, nt, preferred_element_type=F32) * (M_HEAD_DIM ** -0.5)
        e = jnp.exp(sc - jnp.max(sc, axis=-1, keepdims=True))
        pr = e / jnp.sum(e, axis=-1, keepdims=True)
        m_ref[:, lo:hi] = jnp.dot(pr.astype(BF16), vh, preferred_element_type=F32).astype(m_ref.dtype)


def _mixer_prompt(p, cos, sin, mk, mv, w_s, bs_t, ln_v_g, ret_norm_g, decay, q_decay, k_decay, chunk_decay,
                  batch, seq):
    n_chunks = seq // CHUNK
    rows = batch * seq
    qm_tile = (MAIN_W + GATE_W) // M_WIDTH

    def tok(b, c):
        return (b * n_chunks + c, 0)

    const2 = lambda b, c: (0, 0)
    const3 = lambda b, c: (0, 0, 0)
    grid_spec = pl.GridSpec(
        grid=(batch, n_chunks),
        in_specs=[
            pl.BlockSpec(memory_space=pltpu.SMEM),
            pl.BlockSpec((CHUNK, MAIN_W), tok),
            pl.BlockSpec((CHUNK, M_WIDTH), lambda b, c: (b * n_chunks + c, qm_tile)),
            pl.BlockSpec((CHUNK, R_HEAD_DIM), lambda b, c: (c, 0)),
            pl.BlockSpec((CHUNK, R_HEAD_DIM), lambda b, c: (c, 0)),
            pl.BlockSpec((1, N_MEM, M_WIDTH), lambda b, c: (b, 0, 0)),
            pl.BlockSpec((1, N_MEM, M_WIDTH), lambda b, c: (b, 0, 0)),
            pl.BlockSpec((A_GROUPS, CHUNK, CHUNK), const3),
            pl.BlockSpec((CHUNK, A_GROUPS), const2),
            pl.BlockSpec((1, A_WIDTH), const2),
            pl.BlockSpec((1, R_WIDTH), const2),
            pl.BlockSpec((R_HEADS, CHUNK, CHUNK), const3),
            pl.BlockSpec((CHUNK, R_HEADS), const2),
            pl.BlockSpec((CHUNK, R_HEADS), const2),
        ],
        out_specs=[
            pl.BlockSpec((CHUNK, A_WIDTH), tok),
            pl.BlockSpec((CHUNK, R_WIDTH), tok),
            pl.BlockSpec((CHUNK, M_WIDTH), tok),
            pl.BlockSpec((1, 1, CHUNK, A_WIDTH), lambda b, c: (0, b, 0, 0)),
            pl.BlockSpec((1, 1, R_HEADS, R_HEAD_DIM, R_HEAD_DIM), lambda b, c: (0, b, 0, 0, 0)),
        ],
        scratch_shapes=[pltpu.VMEM((R_HEADS, R_HEAD_DIM, R_HEAD_DIM), F32)],
    )
    return pl.pallas_call(
        _mixer_prompt_kernel,
        grid_spec=grid_spec,
        out_shape=[
            jax.ShapeDtypeStruct((rows, A_WIDTH), BF16),
            jax.ShapeDtypeStruct((rows, R_WIDTH), BF16),
            jax.ShapeDtypeStruct((rows, M_WIDTH), BF16),
            jax.ShapeDtypeStruct((1, batch, CHUNK, A_WIDTH), F32),
            jax.ShapeDtypeStruct((1, batch, R_HEADS, R_HEAD_DIM, R_HEAD_DIM), F32),
        ],
        compiler_params=_params(("arbitrary", "arbitrary")),
        name="mixer_prompt",
    )(chunk_decay, p, p, cos, sin, mk, mv, w_s, bs_t, ln_v_g, ret_norm_g, decay, q_decay, k_decay)


def _mixer_sample_kernel(gam_ref, p_ref, cos_ref, sin_ref, ck_ref, cv_ref, s0_ref, w0_ref, b0_ref, lnv_ref, rng_ref,
                         a_ref, b_ref, m_ref, vrows_ref, state_ref):
    bb = p_ref.shape[0]
    v = _ln(p_ref[:, A_WIDTH:2 * A_WIDTH], lnv_ref[...])
    vrows_ref[...] = v
    a_ref[...] = (p_ref[:, 0:A_WIDTH] * (w0_ref[...] * v + b0_ref[...])).astype(a_ref.dtype)

    cos = cos_ref[...]
    sin = sin_ref[...]
    q_off, k_off, v_off, g_off = 2 * A_WIDTH, 2 * A_WIDTH + R_WIDTH, 2 * A_WIDTH + 2 * R_WIDTH, 2 * A_WIDTH + 3 * R_WIDTH
    for h in range(R_HEADS):
        lo, hi = h * R_HEAD_DIM, (h + 1) * R_HEAD_DIM
        gamma = gam_ref[h]
        q = _rotary(p_ref[:, q_off + lo:q_off + hi], cos, sin) * (R_HEAD_DIM ** -0.5)
        k = _rotary(p_ref[:, k_off + lo:k_off + hi], cos, sin)
        vr = p_ref[:, v_off + lo:v_off + hi]
        qt = q.T
        kt = k.T
        qk = jnp.sum(q * k, axis=-1, keepdims=True)
        rows = []
        for b in range(bb):
            s0 = s0_ref[b, h]
            rows.append(jnp.sum(qt[:, b:b + 1] * s0, axis=0, keepdims=True))
            state_ref[0, b, h] = s0 * gamma + kt[:, b:b + 1] * vr[b:b + 1, :]
        o = qk * vr + gamma * jnp.concatenate(rows, axis=0)
        o = _ln(o, rng_ref[:, lo:hi]) * p_ref[:, g_off + lo:g_off + hi]
        b_ref[:, lo:hi] = o.astype(b_ref.dtype)

    for b in range(bb):
        qrow = p_ref[b:b + 1, MAIN_W:MAIN_W + M_WIDTH]
        prod = ck_ref[0, b] * qrow
        outs = []
        for h in range(M_HEADS):
            lo, hi = h * M_HEAD_DIM, (h + 1) * M_HEAD_DIM
            sc = jnp.sum(prod[:, lo:hi], axis=-1, keepdims=True) * (M_HEAD_DIM ** -0.5)
            e = jnp.exp(sc - jnp.max(sc, axis=0, keepdims=True))
            pr = e / jnp.sum(e, axis=0, keepdims=True)
            outs.append(jnp.sum(pr * cv_ref[0, b, :, lo:hi], axis=0, keepdims=True))
        m_ref[b:b + 1, :] = jnp.concatenate(outs, axis=1).astype(m_ref.dtype)


def _mixer_sample(ps, cos, sin, cache_k, cache_v, state, w0, b0, ln_v_g, ret_norm_g, gamma):
    n = ps.shape[0]
    bb = SAMPLE_BB
    const2 = lambda i: (0, 0)
    rowblk = lambda i: (i, 0)
    grid_spec = pl.GridSpec(
        grid=(n // bb,),
        in_specs=[
            pl.BlockSpec(memory_space=pltpu.SMEM),
            pl.BlockSpec((bb, MAIN_W + M_WIDTH), rowblk),
            pl.BlockSpec((1, R_HEAD_DIM), const2),
            pl.BlockSpec((1, R_HEAD_DIM), const2),
            pl.BlockSpec((1, bb, N_MEM, M_WIDTH), lambda i: (0, i, 0, 0)),
            pl.BlockSpec((1, bb, N_MEM, M_WIDTH), lambda i: (0, i, 0, 0)),
            pl.BlockSpec((bb, R_HEADS, R_HEAD_DIM, R_HEAD_DIM), lambda i: (i, 0, 0, 0)),
            pl.BlockSpec((1, A_WIDTH), const2),
            pl.BlockSpec((1, A_WIDTH), const2),
            pl.BlockSpec((1, A_WIDTH), const2),
            pl.BlockSpec((1, R_WIDTH), const2),
        ],
        out_specs=[
            pl.BlockSpec((bb, A_WIDTH), rowblk),
            pl.BlockSpec((bb, R_WIDTH), rowblk),
            pl.BlockSpec((bb, M_WIDTH), rowblk),
            pl.BlockSpec((bb, A_WIDTH), rowblk),
            pl.BlockSpec((1, bb, R_HEADS, R_HEAD_DIM, R_HEAD_DIM), lambda i: (0, i, 0, 0, 0)),
        ],
    )
    return pl.pallas_call(
        _mixer_sample_kernel,
        grid_spec=grid_spec,
        out_shape=[
            jax.ShapeDtypeStruct((n, A_WIDTH), F32),
            jax.ShapeDtypeStruct((n, R_WIDTH), F32),
            jax.ShapeDtypeStruct((n, M_WIDTH), F32),
            jax.ShapeDtypeStruct((n, A_WIDTH), F32),
            jax.ShapeDtypeStruct((1, n, R_HEADS, R_HEAD_DIM, R_HEAD_DIM), F32),
        ],
        compiler_params=_params(("arbitrary",)),
        name="mixer_sample",
    )(gamma, ps, cos, sin, cache_k, cache_v, state, w0, b0, ln_v_g, ret_norm_g)


def _merge_kernel(a_ref, b_ref, m_ref, ga_ref, gb_ref, gm_ref, x_ref, wa_ref, wb_ref, wm_ref, wo_ref,
                  gffn_ref, wrh_ref, wrl_ref, br_ref, y_ref, h2_ref, lg_ref):
    merged = (ga_ref[...].astype(F32) * jnp.dot(a_ref[...], wa_ref[...], preferred_element_type=F32)
              + gb_ref[...].astype(F32) * jnp.dot(b_ref[...], wb_ref[...], preferred_element_type=F32)
              + gm_ref[...].astype(F32) * jnp.dot(m_ref[...], wm_ref[...], preferred_element_type=F32))
    y = x_ref[...] + jnp.dot(merged.astype(BF16), wo_ref[...], preferred_element_type=F32)
    y_ref[...] = y
    h2 = _rms(y, gffn_ref[...])
    hi = h2.astype(BF16)
    h2_ref[...] = hi
    lo = (h2 - hi.astype(F32)).astype(BF16)
    lg_ref[...] = (jnp.dot(hi, wrh_ref[...], preferred_element_type=F32)
                   + jnp.dot(hi, wrl_ref[...], preferred_element_type=F32)
                   + jnp.dot(lo, wrh_ref[...], preferred_element_type=F32)
                   + br_ref[...])


def _merge(a, b, m, p, x, wa, wb, wm, wo, g_ffn, wr_hi, wr_lo, br):
    rows = x.shape[0]
    tm = MERGE_TM
    gate0 = MAIN_W // D_MODEL
    rowblk = lambda i: (i, 0)
    const = lambda i: (0, 0)
    resident = functools.partial(pl.BlockSpec, index_map=const, pipeline_mode=pl.Buffered(1))
    lanes = wr_hi.shape[1]
    return pl.pallas_call(
        _merge_kernel,
        grid=(rows // tm,),
        in_specs=[
            pl.BlockSpec((tm, A_WIDTH), rowblk),
            pl.BlockSpec((tm, R_WIDTH), rowblk),
            pl.BlockSpec((tm, M_WIDTH), rowblk),
            pl.BlockSpec((tm, D_MODEL), lambda i: (i, gate0)),
            pl.BlockSpec((tm, D_MODEL), lambda i: (i, gate0 + 1)),
            pl.BlockSpec((tm, D_MODEL), lambda i: (i, gate0 + 2)),
            pl.BlockSpec((tm, D_MODEL), rowblk),
            resident((A_WIDTH, D_MODEL)),
            resident((R_WIDTH, D_MODEL)),
            resident((M_WIDTH, D_MODEL)),
            resident((D_MODEL, D_MODEL)),
            resident((1, D_MODEL)),
            resident((D_MODEL, lanes)),
            resident((D_MODEL, lanes)),
            resident((1, lanes)),
        ],
        out_specs=[
            pl.BlockSpec((tm, D_MODEL), rowblk),
            pl.BlockSpec((tm, D_MODEL), rowblk),
            pl.BlockSpec((tm, lanes), rowblk),
        ],
        out_shape=[
            jax.ShapeDtypeStruct((rows, D_MODEL), F32),
            jax.ShapeDtypeStruct((rows, D_MODEL), BF16),
            jax.ShapeDtypeStruct((rows, lanes), F32),
        ],
        compiler_params=_params(("arbitrary",)),
        name="merge",
    )(a, b, m, p, p, p, x, wa, wb, wm, wo, g_ffn, wr_hi, wr_lo, br)


def _moe_kernel(te_ref, tv_ref, x_ref, wg_ref, wl_ref, bg_ref, bl_ref, w2_ref, b2_ref, o_ref):
    i = pl.program_id(0)
    j = pl.program_id(1)
    valid = tv_ref[i] > 0

    @pl.when(j == 0)
    def _():
        o_ref[...] = jnp.broadcast_to(b2_ref[0], o_ref.shape)

    @pl.when(valid)
    def _():
        x = x_ref[...]
        glu = jnp.minimum(jnp.dot(x, wg_ref[0], preferred_element_type=F32) + bg_ref[0], SWIGLU_LIMIT)
        lin = jnp.clip(jnp.dot(x, wl_ref[0], preferred_element_type=F32) + bl_ref[0], -SWIGLU_LIMIT, SWIGLU_LIMIT)
        act = glu * jax.nn.sigmoid(SWIGLU_ALPHA * glu) * (lin + 1.0)
        o_ref[...] += jnp.dot(act.astype(BF16), w2_ref[0], preferred_element_type=F32)


def _moe(tile_e, tile_valid, xs, wg, wl, bg, bl, w2, b2):
    n_tiles = tile_e.shape[0]
    nj = D_FF // MOE_TF
    grid_spec = pltpu.PrefetchScalarGridSpec(
        num_scalar_prefetch=2,
        grid=(n_tiles, nj),
        in_specs=[
            pl.BlockSpec((MOE_TM, D_MODEL), lambda i, j, te, tv: (i, 0)),
            pl.BlockSpec((1, D_MODEL, MOE_TF), lambda i, j, te, tv: (te[i], 0, j)),
            pl.BlockSpec((1, D_MODEL, MOE_TF), lambda i, j, te, tv: (te[i], 0, j)),
            pl.BlockSpec((1, 1, MOE_TF), lambda i, j, te, tv: (te[i], 0, j)),
            pl.BlockSpec((1, 1, MOE_TF), lambda i, j, te, tv: (te[i], 0, j)),
            pl.BlockSpec((1, MOE_TF, D_MODEL), lambda i, j, te, tv: (te[i], j, 0)),
            pl.BlockSpec((1, 1, D_MODEL), lambda i, j, te, tv: (te[i], 0, 0)),
        ],
        out_specs=pl.BlockSpec((MOE_TM, D_MODEL), lambda i, j, te, tv: (i, 0)),
    )
    return pl.pallas_call(
        _moe_kernel,
        grid_spec=grid_spec,
        out_shape=jax.ShapeDtypeStruct((n_tiles * MOE_TM, D_MODEL), F32),
        compiler_params=_params(("arbitrary", "arbitrary")),
        name="moe",
    )(tile_e, tile_valid, xs, wg, wl, bg, bl, w2, b2)


def _combine_kernel(y_ref, g_ref, w_ref, gf_ref, o_ref):
    acc = y_ref[...]
    for k in range(TOP_K):
        acc = acc + g_ref[:, k * D_MODEL:(k + 1) * D_MODEL] * w_ref[:, k:k + 1]
    o_ref[...] = _rms(acc, gf_ref[...])


def _combine(y, gathered, top_w, g_final):
    rows = y.shape[0]
    tm = 208
    return pl.pallas_call(
        _combine_kernel,
        grid=(rows // tm,),
        in_specs=[
            pl.BlockSpec((tm, D_MODEL), lambda i: (i, 0)),
            pl.BlockSpec((tm, TOP_K * D_MODEL), lambda i: (i, 0)),
            pl.BlockSpec((tm, TOP_K), lambda i: (i, 0)),
            pl.BlockSpec((1, D_MODEL), lambda i: (0, 0)),
        ],
        out_specs=pl.BlockSpec((tm, D_MODEL), lambda i: (i, 0)),
        out_shape=jax.ShapeDtypeStruct((rows, D_MODEL), F32),
        compiler_params=_params(("arbitrary",)),
        name="combine",
    )(y, gathered, top_w, g_final)


def _route(logits, n_tok):
    top_val, top_idx = lax.top_k(logits, TOP_K)
    top_w = jax.nn.softmax(top_val, axis=-1)
    n_asg = n_tok * TOP_K
    flat_e = top_idx.reshape(n_asg).astype(jnp.int32)
    order = jnp.argsort(flat_e)
    sorted_e = flat_e[order]
    counts = jnp.bincount(flat_e, length=N_EXPERTS).astype(jnp.int32)
    padded = ((counts + MOE_TM - 1) // MOE_TM) * MOE_TM
    pad_end = jnp.cumsum(padded)
    pad_start = pad_end - padded
    start = jnp.cumsum(counts) - counts
    dest_sorted = pad_start[sorted_e] + jnp.arange(n_asg, dtype=jnp.int32) - start[sorted_e]
    n_tiles = -(-n_asg // MOE_TM) + N_EXPERTS
    buf_tok = jnp.zeros((n_tiles * MOE_TM,), jnp.int32).at[dest_sorted].set((order // TOP_K).astype(jnp.int32))
    dest = jnp.zeros((n_asg,), jnp.int32).at[order].set(dest_sorted).reshape(n_tok, TOP_K)
    tile_start = jnp.arange(n_tiles, dtype=jnp.int32) * MOE_TM
    tile_valid = (tile_start < pad_end[-1]).astype(jnp.int32)
    tile_e = jnp.minimum(jnp.searchsorted(pad_end, tile_start, side="right"), N_EXPERTS - 1).astype(jnp.int32)
    n_used = pad_end[-1] // MOE_TM
    tile_e = jnp.where(tile_valid > 0, tile_e, tile_e[jnp.maximum(n_used - 1, 0)])
    return top_w, buf_tok, dest, tile_e, tile_valid


def kernel(x_prompt, x_sample, cache_mem_k, cache_mem_v, state_ret, mem_prompt, g_mix, w_in, ln_v_g, w_s, b_s,
           ret_norm_g, g_mem, w_mk, w_mv, w_a, w_b, w_m, w_o, g_ffn, w_router, b_router, w1, b1, w2, b2, g_final):
    batch, seq, _ = x_prompt.shape
    n_dec = x_sample.shape[0]
    n_prompt_tok = batch * seq
    n_tok = n_prompt_tok + n_dec
    l = 0

    x = jnp.concatenate([x_prompt.reshape(n_prompt_tok, D_MODEL), x_sample.reshape(n_dec, D_MODEL)], axis=0)

    half = R_HEAD_DIM // 2
    inv = ROPE_BASE ** (-jnp.arange(half, dtype=F32) / half)

    def rope_tables(pos):
        ang = pos[:, None] * inv[None, :]
        cos, sin = jnp.cos(ang), jnp.sin(ang)
        return jnp.concatenate([cos, cos], axis=1), jnp.concatenate([-sin, sin], axis=1)

    cos_p, sin_p = rope_tables(jnp.arange(seq, dtype=F32))
    cos_s, sin_s = rope_tables(PAST_LEN + jnp.arange(1, dtype=F32))
    log_gamma = jnp.log(1.0 - 2.0 ** (-5.0 - jnp.arange(R_HEADS, dtype=F32)))
    idx = jnp.arange(CHUNK, dtype=F32)
    diff = idx[:, None] - idx[None, :]
    decay = jnp.where(diff[None] >= 0, jnp.exp(jnp.maximum(diff, 0.0)[None] * log_gamma[:, None, None]), 0.0)
    q_decay = jnp.exp((idx[:, None] + 1.0) * log_gamma[None, :])
    k_decay = jnp.exp((CHUNK - 1.0 - idx[:, None]) * log_gamma[None, :])
    chunk_decay = jnp.exp(CHUNK * log_gamma)
    gamma = jnp.exp(log_gamma)

    mk, mv = _memkv(mem_prompt.reshape(batch * N_MEM, D_MODEL), g_mem[l][None], w_mk[l].astype(BF16),
                    w_mv[l].astype(BF16))
    mk3 = mk.reshape(batch, N_MEM, M_WIDTH)
    mv3 = mv.reshape(batch, N_MEM, M_WIDTH)

    p = _inproj(x, g_mix[l][None], w_in[l].astype(BF16))

    a_p, b_p, m_p, vrows_p, state_p = _mixer_prompt(
        p, cos_p, sin_p, mk3, mv3, w_s[l], b_s[l].T, ln_v_g[l][None], ret_norm_g[l][None],
        decay, q_decay, k_decay, chunk_decay, batch, seq)

    ps = jnp.concatenate([p[n_prompt_tok:, :MAIN_W], p[n_prompt_tok:, MAIN_W + GATE_W:]], axis=1).astype(F32)
    w0 = jnp.repeat(w_s[l][:, 0, 0], A_GROUP_W)[None]
    b0 = jnp.repeat(b_s[l][:, 0], A_GROUP_W)[None]
    a_s, b_s_out, m_s, vrows_s, state_s = _mixer_sample(
        ps, cos_s, sin_s, cache_mem_k.reshape(1, n_dec, N_MEM, M_WIDTH), cache_mem_v.reshape(1, n_dec, N_MEM, M_WIDTH),
        state_ret[l], w0, b0, ln_v_g[l][None], ret_norm_g[l][None], gamma)

    a = jnp.concatenate([a_p, a_s.astype(BF16)], axis=0)
    b = jnp.concatenate([b_p, b_s_out.astype(BF16)], axis=0)
    m = jnp.concatenate([m_p, m_s.astype(BF16)], axis=0)

    lanes = 128
    wr = jnp.zeros((D_MODEL, lanes), F32).at[:, :N_EXPERTS].set(w_router[l])
    wr_hi = wr.astype(BF16)
    wr_lo = (wr - wr_hi.astype(F32)).astype(BF16)
    br = jnp.zeros((1, lanes), F32).at[0, :N_EXPERTS].set(b_router[l])
    y, h2, logits = _merge(a, b, m, p, x, w_a[l].astype(BF16), w_b[l].astype(BF16), w_m[l].astype(BF16),
                           w_o[l].astype(BF16), g_ffn[l][None], wr_hi, wr_lo, br)

    top_w, buf_tok, dest, tile_e, tile_valid = _route(logits[:, :N_EXPERTS], n_tok)
    xs = h2[buf_tok]
    w1l = w1[l]
    ys = _moe(tile_e, tile_valid, xs, w1l[:, :, 0::2].astype(BF16), w1l[:, :, 1::2].astype(BF16),
              b1[l][:, None, 0::2], b1[l][:, None, 1::2], w2[l].astype(BF16), b2[l][:, None, :])
    out = _combine(y, ys[dest].reshape(n_tok, TOP_K * D_MODEL), top_w, g_final[None])

    y_prompt = out[:n_prompt_tok].reshape(batch, seq, D_MODEL)
    y_sample = out[n_prompt_tok:].reshape(n_dec, 1, D_MODEL)
    return (y_prompt, y_sample,
            mk.reshape(1, batch, N_MEM, M_HEADS, M_HEAD_DIM), mv.reshape(1, batch, N_MEM, M_HEADS, M_HEAD_DIM),
            state_p, vrows_p, state_s, vrows_s.reshape(1, n_dec, 1, A_WIDTH))
```

```python
import functools

import jax
import jax.numpy as jnp
import numpy as np
from jax import lax
from jax.experimental import pallas as pl
from jax.experimental.pallas import tpu as pltpu

F32 = jnp.float32
BF16 = jnp.bfloat16

D_MODEL = 2048
CHUNK = 128
A_WIDTH = 1024
A_GROUPS = 4
A_GROUP_W = A_WIDTH // A_GROUPS
R_HEADS = 8
R_WIDTH = 1024
R_HEAD_DIM = 128
ROPE_BASE = 10000.0
M_HEADS = 4
M_HEAD_DIM = 128
M_WIDTH = 512
N_MEM = 256
N_EXPERTS = 32
TOP_K = 4
D_FF = 2048
SWIGLU_ALPHA = 1.702
SWIGLU_LIMIT = 7.0
EPS = 1e-6
PAST_LEN = 16384

MAIN_W = 2 * A_WIDTH + 4 * R_WIDTH
GATE_W = 3 * D_MODEL
IN_WIDTH = MAIN_W + M_WIDTH + GATE_W

V7X_VMEM_BYTES = 64 * 1024 * 1024
VMEM_LIMIT = V7X_VMEM_BYTES - 8 * 1024 * 1024

IN_TN = 512
IN_TM = 1040
MERGE_TM = 320
MOE_TM = 256
MOE_UP_TN = 1024
MOE_DOWN_TN = 1024
SAMPLE_BB = 8


def _params(sem):
    return pltpu.CompilerParams(dimension_semantics=sem, vmem_limit_bytes=VMEM_LIMIT)


def _rms(x, g):
    return x * lax.rsqrt(jnp.mean(x * x, axis=-1, keepdims=True) + EPS) * g


def _ln(x, g):
    xc = x - jnp.mean(x, axis=-1, keepdims=True)
    return xc * lax.rsqrt(jnp.mean(xc * xc, axis=-1, keepdims=True) + EPS) * g


def _memkv_kernel(m_ref, g_ref, wk_ref, wv_ref, k_ref, v_ref):
    m = _rms(m_ref[...], g_ref[...]).astype(BF16)
    k_ref[...] = jnp.dot(m, wk_ref[...], preferred_element_type=F32)
    v_ref[...] = jnp.dot(m, wv_ref[...], preferred_element_type=F32)


def _memkv(mem, g_mem, wk, wv):
    rows = mem.shape[0]
    tm = 256
    return pl.pallas_call(
        _memkv_kernel,
        grid=(rows // tm,),
        in_specs=[
            pl.BlockSpec((tm, D_MODEL), lambda i: (i, 0)),
            pl.BlockSpec((1, D_MODEL), lambda i: (0, 0)),
            pl.BlockSpec((D_MODEL, M_WIDTH), lambda i: (0, 0)),
            pl.BlockSpec((D_MODEL, M_WIDTH), lambda i: (0, 0)),
        ],
        out_specs=[
            pl.BlockSpec((tm, M_WIDTH), lambda i: (i, 0)),
            pl.BlockSpec((tm, M_WIDTH), lambda i: (i, 0)),
        ],
        out_shape=[jax.ShapeDtypeStruct((rows, M_WIDTH), F32)] * 2,
        compiler_params=_params(("arbitrary",)),
        name="memkv",
    )(mem, g_mem, wk, wv)


_N_MAIN_T = MAIN_W // IN_TN
_N_GATE_T = GATE_W // IN_TN
_N_IN_T = IN_WIDTH // IN_TN
_QM_SRC_T = MAIN_W // IN_TN


def _in_src_tile(j):
    return jnp.where(j < _N_MAIN_T, j, jnp.where(j < _N_MAIN_T + _N_GATE_T, j + 1, _QM_SRC_T))


def _inproj_kernel(x_ref, g_ref, w_ref, o_ref, h_scr):
    j = pl.program_id(1)

    @pl.when(j == 0)
    def _():
        h_scr[...] = _rms(x_ref[...], g_ref[...]).astype(BF16)

    acc = jnp.dot(h_scr[...], w_ref[...], preferred_element_type=F32)
    n_gelu = 2 * A_WIDTH // IN_TN
    silu_lo = (2 * A_WIDTH + 3 * R_WIDTH) // IN_TN

    @pl.when(j < n_gelu)
    def _():
        o_ref[...] = jax.nn.gelu(acc).astype(o_ref.dtype)

    @pl.when(jnp.logical_or(jnp.logical_and(j >= n_gelu, j < silu_lo), j == _N_IN_T - 1))
    def _():
        o_ref[...] = acc.astype(o_ref.dtype)

    @pl.when(jnp.logical_and(j >= silu_lo, j < _N_MAIN_T))
    def _():
        o_ref[...] = (acc * jax.nn.sigmoid(acc)).astype(o_ref.dtype)

    @pl.when(jnp.logical_and(j >= _N_MAIN_T, j < _N_IN_T - 1))
    def _():
        o_ref[...] = jax.nn.sigmoid(acc).astype(o_ref.dtype)


def _inproj(x, g_mix, w_in):
    rows = x.shape[0]
    return pl.pallas_call(
        _inproj_kernel,
        grid=(rows // IN_TM, _N_IN_T),
        in_specs=[
            pl.BlockSpec((IN_TM, D_MODEL), lambda i, j: (i, 0)),
            pl.BlockSpec((1, D_MODEL), lambda i, j: (0, 0)),
            pl.BlockSpec((D_MODEL, IN_TN), lambda i, j: (0, _in_src_tile(j))),
        ],
        out_specs=pl.BlockSpec((IN_TM, IN_TN), lambda i, j: (i, j)),
        out_shape=jax.ShapeDtypeStruct((rows, IN_WIDTH), BF16),
        scratch_shapes=[pltpu.VMEM((IN_TM, D_MODEL), BF16)],
        compiler_params=_params(("arbitrary", "arbitrary")),
        name="inproj",
    )(x, g_mix, w_in)


def _rotary(x, cos, sin_signed):
    return x * cos + pltpu.roll(x, R_HEAD_DIM // 2, axis=1) * sin_signed


def _mixer_prompt_kernel(cd_ref, p_ref, qm_ref, cos_ref, sin_ref, mk_ref, mv_ref, ws_ref, bs_ref,
                         lnv_ref, rng_ref, decay_ref, qd_ref, kd_ref,
                         a_ref, b_ref, m_ref, vrows_ref, state_ref, s_scr):
    c = pl.program_id(1)

    @pl.when(c == 0)
    def _():
        s_scr[...] = jnp.zeros_like(s_scr)

    v = _ln(p_ref[:, A_WIDTH:2 * A_WIDTH].astype(F32), lnv_ref[...])
    vrows_ref[0, 0] = v
    vb = v.astype(BF16)
    row = lax.broadcasted_iota(jnp.int32, (CHUNK, CHUNK), 0)
    col = lax.broadcasted_iota(jnp.int32, (CHUNK, CHUNK), 1)
    causal = row >= col
    for g in range(A_GROUPS):
        lo, hi = g * A_GROUP_W, (g + 1) * A_GROUP_W
        w = jnp.where(causal, ws_ref[g], 0.0).astype(BF16)
        s = jnp.dot(w, vb[:, lo:hi], preferred_element_type=F32) + bs_ref[:, g:g + 1]
        a_ref[:, lo:hi] = (p_ref[:, lo:hi].astype(F32) * s).astype(a_ref.dtype)

    cos = cos_ref[...]
    sin = sin_ref[...]
    q_off, k_off, v_off, g_off = 2 * A_WIDTH, 2 * A_WIDTH + R_WIDTH, 2 * A_WIDTH + 2 * R_WIDTH, 2 * A_WIDTH + 3 * R_WIDTH
    nt = (((1,), (1,)), ((), ()))
    tn = (((0,), (0,)), ((), ()))
    for h in range(R_HEADS):
        lo, hi = h * R_HEAD_DIM, (h + 1) * R_HEAD_DIM
        q = _rotary(p_ref[:, q_off + lo:q_off + hi].astype(F32), cos, sin) * (R_HEAD_DIM ** -0.5)
        k = _rotary(p_ref[:, k_off + lo:k_off + hi].astype(F32), cos, sin)
        vr = p_ref[:, v_off + lo:v_off + hi]
        s0 = s_scr[h]
        scores = lax.dot_general(q.astype(BF16), k.astype(BF16), nt, preferred_element_type=F32) * decay_ref[h]
        o = (jnp.dot(scores.astype(BF16), vr, preferred_element_type=F32)
             + jnp.dot((q * qd_ref[:, h:h + 1]).astype(BF16), s0.astype(BF16), preferred_element_type=F32))
        s_scr[h] = s0 * cd_ref[h] + lax.dot_general((k * kd_ref[:, h:h + 1]).astype(BF16), vr, tn,
                                                     preferred_element_type=F32)
        o = _ln(o, rng_ref[:, lo:hi]) * p_ref[:, g_off + lo:g_off + hi].astype(F32)
        b_ref[:, lo:hi] = o.astype(b_ref.dtype)

    @pl.when(c == pl.num_programs(1) - 1)
    def _():
        state_ref[0, 0] = s_scr[...]

    for h in range(M_HEADS):
        lo, hi = h * M_HEAD_DIM, (h + 1) * M_HEAD_DIM
        kh = mk_ref[0, :, lo:hi].astype(BF16)
        vh = mv_ref[0, :, lo:hi].astype(BF16)
        sc = lax.dot_general(qm_ref[:, lo:hi], kh, nt, preferred_element_type=F32) * (M_HEAD_DIM ** -0.5)
        e = jnp.exp(sc - jnp.max(sc, axis=-1, keepdims=True))
        pr = e / jnp.sum(e, axis=-1, keepdims=True)
        m_ref[:, lo:hi] = jnp.dot(pr.astype(BF16), vh, preferred_element_type=F32).astype(m_ref.dtype)


def _mixer_prompt(p, cos, sin, mk, mv, w_s, bs_t, ln_v_g, ret_norm_g, decay, q_decay, k_decay, chunk_decay,
                  batch, seq):
    n_chunks = seq // CHUNK
    rows = batch * seq
    qm_tile = (MAIN_W + GATE_W) // M_WIDTH

    def tok(b, c):
        return (b * n_chunks + c, 0)

    const2 = lambda b, c: (0, 0)
    const3 = lambda b, c: (0, 0, 0)
    grid_spec = pl.GridSpec(
        grid=(batch, n_chunks),
        in_specs=[
            pl.BlockSpec(memory_space=pltpu.SMEM),
            pl.BlockSpec((CHUNK, MAIN_W), tok),
            pl.BlockSpec((CHUNK, M_WIDTH), lambda b, c: (b * n_chunks + c, qm_tile)),
            pl.BlockSpec((CHUNK, R_HEAD_DIM), lambda b, c: (c, 0)),
            pl.BlockSpec((CHUNK, R_HEAD_DIM), lambda b, c: (c, 0)),
            pl.BlockSpec((1, N_MEM, M_WIDTH), lambda b, c: (b, 0, 0)),
            pl.BlockSpec((1, N_MEM, M_WIDTH), lambda b, c: (b, 0, 0)),
            pl.BlockSpec((A_GROUPS, CHUNK, CHUNK), const3),
            pl.BlockSpec((CHUNK, A_GROUPS), const2),
            pl.BlockSpec((1, A_WIDTH), const2),
            pl.BlockSpec((1, R_WIDTH), const2),
            pl.BlockSpec((R_HEADS, CHUNK, CHUNK), const3),
            pl.BlockSpec((CHUNK, R_HEADS), const2),
            pl.BlockSpec((CHUNK, R_HEADS), const2),
        ],
        out_specs=[
            pl.BlockSpec((CHUNK, A_WIDTH), tok),
            pl.BlockSpec((CHUNK, R_WIDTH), tok),
            pl.BlockSpec((CHUNK, M_WIDTH), tok),
            pl.BlockSpec((1, 1, CHUNK, A_WIDTH), lambda b, c: (0, b, 0, 0)),
            pl.BlockSpec((1, 1, R_HEADS, R_HEAD_DIM, R_HEAD_DIM), lambda b, c: (0, b, 0, 0, 0)),
        ],
        scratch_shapes=[pltpu.VMEM((R_HEADS, R_HEAD_DIM, R_HEAD_DIM), F32)],
    )
    return pl.pallas_call(
        _mixer_prompt_kernel,
        grid_spec=grid_spec,
        out_shape=[
            jax.ShapeDtypeStruct((rows, A_WIDTH), BF16),
            jax.ShapeDtypeStruct((rows, R_WIDTH), BF16),
            jax.ShapeDtypeStruct((rows, M_WIDTH), BF16),
            jax.ShapeDtypeStruct((1, batch, CHUNK, A_WIDTH), F32),
            jax.ShapeDtypeStruct((1, batch, R_HEADS, R_HEAD_DIM, R_HEAD_DIM), F32),
        ],
        compiler_params=_params(("arbitrary", "arbitrary")),
        name="mixer_prompt",
    )(chunk_decay, p, p, cos, sin, mk, mv, w_s, bs_t, ln_v_g, ret_norm_g, decay, q_decay, k_decay)


def _mixer_sample_kernel(gam_ref, p_ref, cos_ref, sin_ref, ck_ref, cv_ref, s0_ref, w0_ref, b0_ref, lnv_ref, rng_ref,
                         a_ref, b_ref, m_ref, vrows_ref, state_ref):
    bb = p_ref.shape[0]
    v = _ln(p_ref[:, A_WIDTH:2 * A_WIDTH], lnv_ref[...])
    vrows_ref[...] = v
    a_ref[...] = (p_ref[:, 0:A_WIDTH] * (w0_ref[...] * v + b0_ref[...])).astype(a_ref.dtype)

    cos = cos_ref[...]
    sin = sin_ref[...]
    q_off, k_off, v_off, g_off = 2 * A_WIDTH, 2 * A_WIDTH + R_WIDTH, 2 * A_WIDTH + 2 * R_WIDTH, 2 * A_WIDTH + 3 * R_WIDTH
    for h in range(R_HEADS):
        lo, hi = h * R_HEAD_DIM, (h + 1) * R_HEAD_DIM
        gamma = gam_ref[h]
        q = _rotary(p_ref[:, q_off + lo:q_off + hi], cos, sin) * (R_HEAD_DIM ** -0.5)
        k = _rotary(p_ref[:, k_off + lo:k_off + hi], cos, sin)
        vr = p_ref[:, v_off + lo:v_off + hi]
        qt = q.T
        kt = k.T
        qk = jnp.sum(q * k, axis=-1, keepdims=True)
        rows = []
        for b in range(bb):
            s0 = s0_ref[b, h]
            rows.append(jnp.sum(qt[:, b:b + 1] * s0, axis=0, keepdims=True))
            state_ref[0, b, h] = s0 * gamma + kt[:, b:b + 1] * vr[b:b + 1, :]
        o = qk * vr + gamma * jnp.concatenate(rows, axis=0)
        o = _ln(o, rng_ref[:, lo:hi]) * p_ref[:, g_off + lo:g_off + hi]
        b_ref[:, lo:hi] = o.astype(b_ref.dtype)

    for b in range(bb):
        qrow = p_ref[b:b + 1, MAIN_W:MAIN_W + M_WIDTH]
        prod = ck_ref[0, b] * qrow
        outs = []
        for h in range(M_HEADS):
            lo, hi = h * M_HEAD_DIM, (h + 1) * M_HEAD_DIM
            sc = jnp.sum(prod[:, lo:hi], axis=-1, keepdims=True) * (M_HEAD_DIM ** -0.5)
            e = jnp.exp(sc - jnp.max(sc, axis=0, keepdims=True))
            pr = e / jnp.sum(e, axis=0, keepdims=True)
            outs.append(jnp.sum(pr * cv_ref[0, b, :, lo:hi], axis=0, keepdims=True))
        m_ref[b:b + 1, :] = jnp.concatenate(outs, axis=1).astype(m_ref.dtype)


def _mixer_sample(ps, cos, sin, cache_k, cache_v, state, w0, b0, ln_v_g, ret_norm_g, gamma):
    n = ps.shape[0]
    bb = SAMPLE_BB
    const2 = lambda i: (0, 0)
    rowblk = lambda i: (i, 0)
    grid_spec = pl.GridSpec(
        grid=(n // bb,),
        in_specs=[
            pl.BlockSpec(memory_space=pltpu.SMEM),
            pl.BlockSpec((bb, MAIN_W + M_WIDTH), rowblk),
            pl.BlockSpec((1, R_HEAD_DIM), const2),
            pl.BlockSpec((1, R_HEAD_DIM), const2),
            pl.BlockSpec((1, bb, N_MEM, M_WIDTH), lambda i: (0, i, 0, 0)),
            pl.BlockSpec((1, bb, N_MEM, M_WIDTH), lambda i: (0, i, 0, 0)),
            pl.BlockSpec((bb, R_HEADS, R_HEAD_DIM, R_HEAD_DIM), lambda i: (i, 0, 0, 0)),
            pl.BlockSpec((1, A_WIDTH), const2),
            pl.BlockSpec((1, A_WIDTH), const2),
            pl.BlockSpec((1, A_WIDTH), const2),
            pl.BlockSpec((1, R_WIDTH), const2),
        ],
        out_specs=[
            pl.BlockSpec((bb, A_WIDTH), rowblk),
            pl.BlockSpec((bb, R_WIDTH), rowblk),
            pl.BlockSpec((bb, M_WIDTH), rowblk),
            pl.BlockSpec((bb, A_WIDTH), rowblk),
            pl.BlockSpec((1, bb, R_HEADS, R_HEAD_DIM, R_HEAD_DIM), lambda i: (0, i, 0, 0, 0)),
        ],
    )
    return pl.pallas_call(
        _mixer_sample_kernel,
        grid_spec=grid_spec,
        out_shape=[
            jax.ShapeDtypeStruct((n, A_WIDTH), F32),
            jax.ShapeDtypeStruct((n, R_WIDTH), F32),
            jax.ShapeDtypeStruct((n, M_WIDTH), F32),
            jax.ShapeDtypeStruct((n, A_WIDTH), F32),
            jax.ShapeDtypeStruct((1, n, R_HEADS, R_HEAD_DIM, R_HEAD_DIM), F32),
        ],
        compiler_params=_params(("arbitrary",)),
        name="mixer_sample",
    )(gamma, ps, cos, sin, cache_k, cache_v, state, w0, b0, ln_v_g, ret_norm_g)


def _merge_kernel(a_ref, b_ref, m_ref, ga_ref, gb_ref, gm_ref, x_ref, wa_ref, wb_ref, wm_ref, wo_ref,
                  gffn_ref, wrh_ref, wrl_ref, br_ref, y_ref, h2_ref, lg_ref):
    merged = (ga_ref[...].astype(F32) * jnp.dot(a_ref[...], wa_ref[...], preferred_element_type=F32)
              + gb_ref[...].astype(F32) * jnp.dot(b_ref[...], wb_ref[...], preferred_element_type=F32)
              + gm_ref[...].astype(F32) * jnp.dot(m_ref[...], wm_ref[...], preferred_element_type=F32))
    y = x_ref[...] + jnp.dot(merged.astype(BF16), wo_ref[...], preferred_element_type=F32)
    y_ref[...] = y
    h2 = _rms(y, gffn_ref[...])
    hi = h2.astype(BF16)
    hi_f = hi.astype(F32)
    half = D_MODEL // 2
    h2_ref[...] = ((pltpu.bitcast(hi_f[:, :half], jnp.uint32) >> 16)
                   | (pltpu.bitcast(hi_f[:, half:], jnp.uint32) & jnp.uint32(0xFFFF0000)))
    lo = (h2 - hi_f).astype(BF16)
    lg_ref[...] = (jnp.dot(hi, wrh_ref[...], preferred_element_type=F32)
                   + jnp.dot(hi, wrl_ref[...], preferred_element_type=F32)
                   + jnp.dot(lo, wrh_ref[...], preferred_element_type=F32)
                   + br_ref[...])


def _merge(a, b, m, p, x, wa, wb, wm, wo, g_ffn, wr_hi, wr_lo, br):
    rows = x.shape[0]
    tm = MERGE_TM
    gate0 = MAIN_W // D_MODEL
    rowblk = lambda i: (i, 0)
    const = lambda i: (0, 0)
    resident = functools.partial(pl.BlockSpec, index_map=const, pipeline_mode=pl.Buffered(1))
    lanes = wr_hi.shape[1]
    return pl.pallas_call(
        _merge_kernel,
        grid=(rows // tm,),
        in_specs=[
            pl.BlockSpec((tm, A_WIDTH), rowblk),
            pl.BlockSpec((tm, R_WIDTH), rowblk),
            pl.BlockSpec((tm, M_WIDTH), rowblk),
            pl.BlockSpec((tm, D_MODEL), lambda i: (i, gate0)),
            pl.BlockSpec((tm, D_MODEL), lambda i: (i, gate0 + 1)),
            pl.BlockSpec((tm, D_MODEL), lambda i: (i, gate0 + 2)),
            pl.BlockSpec((tm, D_MODEL), rowblk),
            resident((A_WIDTH, D_MODEL)),
            resident((R_WIDTH, D_MODEL)),
            resident((M_WIDTH, D_MODEL)),
            resident((D_MODEL, D_MODEL)),
            resident((1, D_MODEL)),
            resident((D_MODEL, lanes)),
            resident((D_MODEL, lanes)),
            resident((1, lanes)),
        ],
        out_specs=[
            pl.BlockSpec((tm, D_MODEL), rowblk),
            pl.BlockSpec((tm, D_MODEL // 2), rowblk),
            pl.BlockSpec((tm, lanes), rowblk),
        ],
        out_shape=[
            jax.ShapeDtypeStruct((rows, D_MODEL), F32),
            jax.ShapeDtypeStruct((rows, D_MODEL // 2), jnp.uint32),
            jax.ShapeDtypeStruct((rows, lanes), F32),
        ],
        compiler_params=_params(("arbitrary",)),
        name="merge",
    )(a, b, m, p, p, p, x, wa, wb, wm, wo, g_ffn, wr_hi, wr_lo, br)


def _expert_changed(te_ref, i):
    return jnp.logical_or(i == 0, te_ref[i] != te_ref[jnp.maximum(i - 1, 0)])


def _moe_up_kernel(te_ref, tv_ref, x_ref, w1_ref, b1_ref, sel_ref, act_ref, w_scr):
    i = pl.program_id(1)

    @pl.when(_expert_changed(te_ref, i))
    def _():
        w_scr[...] = w1_ref[0].astype(BF16)

    @pl.when(tv_ref[i] > 0)
    def _():
        half = D_MODEL // 2
        xw = x_ref[...]
        x_lo = pltpu.bitcast(xw << 16, F32).astype(BF16)
        x_hi = pltpu.bitcast(xw & jnp.uint32(0xFFFF0000), F32).astype(BF16)
        hh = (jnp.dot(x_lo, w_scr[:half, :], preferred_element_type=F32)
              + jnp.dot(x_hi, w_scr[half:, :], preferred_element_type=F32) + b1_ref[0])
        glu = jnp.minimum(hh, SWIGLU_LIMIT)
        gate = glu * jax.nn.sigmoid(SWIGLU_ALPHA * glu)
        lin = jnp.clip(hh, -SWIGLU_LIMIT, SWIGLU_LIMIT) + 1.0
        lanes = 128
        for blk in range(MOE_UP_TN // (2 * lanes)):
            parts = []
            for s in range(2):
                lo = (2 * blk + s) * lanes
                parts.append(gate[:, lo:lo + lanes] * pltpu.roll(lin[:, lo:lo + lanes], lanes - 1, axis=1))
            pair = jnp.concatenate(parts, axis=1).astype(BF16)
            act_ref[:, blk * lanes:(blk + 1) * lanes] = jnp.dot(
                pair, sel_ref[...], preferred_element_type=F32).astype(act_ref.dtype)

    @pl.when(tv_ref[i] == 0)
    def _():
        act_ref[...] = jnp.zeros_like(act_ref)


def _moe_down_kernel(te_ref, tv_ref, a_ref, w2_ref, b2_ref, o_ref, w_scr):
    i = pl.program_id(1)

    @pl.when(_expert_changed(te_ref, i))
    def _():
        w_scr[...] = w2_ref[0].astype(BF16)

    o_ref[...] = jnp.dot(a_ref[...], w_scr[...], preferred_element_type=F32) + b2_ref[0]


def _moe(tile_e, tile_valid, xs, w1, b1, w2, b2):
    n_tiles = tile_e.shape[0]
    rows = n_tiles * MOE_TM
    sel = (jnp.arange(256, dtype=jnp.int32)[:, None] == 2 * jnp.arange(128, dtype=jnp.int32)[None, :]).astype(BF16)
    up_spec = pltpu.PrefetchScalarGridSpec(
        num_scalar_prefetch=2,
        grid=(2 * D_FF // MOE_UP_TN, n_tiles),
        in_specs=[
            pl.BlockSpec((MOE_TM, D_MODEL // 2), lambda j, i, te, tv: (i, 0)),
            pl.BlockSpec((1, D_MODEL, MOE_UP_TN), lambda j, i, te, tv: (te[i], 0, j)),
            pl.BlockSpec((1, 1, MOE_UP_TN), lambda j, i, te, tv: (te[i], 0, j)),
            pl.BlockSpec((256, 128), lambda j, i, te, tv: (0, 0)),
        ],
        out_specs=pl.BlockSpec((MOE_TM, MOE_UP_TN // 2), lambda j, i, te, tv: (i, j)),
        scratch_shapes=[pltpu.VMEM((D_MODEL, MOE_UP_TN), BF16)],
    )
    act = pl.pallas_call(
        _moe_up_kernel,
        grid_spec=up_spec,
        out_shape=jax.ShapeDtypeStruct((rows, D_FF), BF16),
        compiler_params=_params(("arbitrary", "arbitrary")),
        name="moe_up",
    )(tile_e, tile_valid, xs, w1, b1, sel)
    down_spec = pltpu.PrefetchScalarGridSpec(
        num_scalar_prefetch=2,
        grid=(D_MODEL // MOE_DOWN_TN, n_tiles),
        in_specs=[
            pl.BlockSpec((MOE_TM, D_FF), lambda j, i, te, tv: (i, 0)),
            pl.BlockSpec((1, D_FF, MOE_DOWN_TN), lambda j, i, te, tv: (te[i], 0, j)),
            pl.BlockSpec((1, 1, MOE_DOWN_TN), lambda j, i, te, tv: (te[i], 0, j)),
        ],
        out_specs=pl.BlockSpec((MOE_TM, MOE_DOWN_TN), lambda j, i, te, tv: (i, j)),
        scratch_shapes=[pltpu.VMEM((D_FF, MOE_DOWN_TN), BF16)],
    )
    return pl.pallas_call(
        _moe_down_kernel,
        grid_spec=down_spec,
        out_shape=jax.ShapeDtypeStruct((rows, D_MODEL), F32),
        compiler_params=_params(("arbitrary", "arbitrary")),
        name="moe_down",
    )(tile_e, tile_valid, act, w2, b2)


def _combine_kernel(y_ref, g_ref, w_ref, gf_ref, o_ref):
    acc = y_ref[...]
    for k in range(TOP_K):
        acc = acc + g_ref[k] * w_ref[:, k:k + 1]
    o_ref[...] = _rms(acc, gf_ref[...])


def _combine(y, gathered, top_w, g_final):
    rows = y.shape[0]
    tm = 208
    return pl.pallas_call(
        _combine_kernel,
        grid=(rows // tm,),
        in_specs=[
            pl.BlockSpec((tm, D_MODEL), lambda i: (i, 0)),
            pl.BlockSpec((TOP_K, tm, D_MODEL), lambda i: (0, i, 0)),
            pl.BlockSpec((tm, TOP_K), lambda i: (i, 0)),
            pl.BlockSpec((1, D_MODEL), lambda i: (0, 0)),
        ],
        out_specs=pl.BlockSpec((tm, D_MODEL), lambda i: (i, 0)),
        out_shape=jax.ShapeDtypeStruct((rows, D_MODEL), F32),
        compiler_params=_params(("arbitrary",)),
        name="combine",
    )(y, gathered, top_w, g_final)


def _route(logits, n_tok):
    top_val, top_idx = lax.top_k(logits, TOP_K)
    top_w = jax.nn.softmax(top_val, axis=-1)
    n_asg = n_tok * TOP_K
    flat_e = top_idx.reshape(n_asg).astype(jnp.int32)
    order = jnp.argsort(flat_e)
    sorted_e = flat_e[order]
    counts = jnp.bincount(flat_e, length=N_EXPERTS).astype(jnp.int32)
    padded = ((counts + MOE_TM - 1) // MOE_TM) * MOE_TM
    pad_end = jnp.cumsum(padded)
    pad_start = pad_end - padded
    start = jnp.cumsum(counts) - counts
    dest_sorted = pad_start[sorted_e] + jnp.arange(n_asg, dtype=jnp.int32) - start[sorted_e]
    n_tiles = -(-n_asg // MOE_TM) + N_EXPERTS
    buf_tok = jnp.zeros((n_tiles * MOE_TM,), jnp.int32).at[dest_sorted].set((order // TOP_K).astype(jnp.int32))
    dest = jnp.zeros((n_asg,), jnp.int32).at[order].set(dest_sorted).reshape(n_tok, TOP_K).T
    tile_start = jnp.arange(n_tiles, dtype=jnp.int32) * MOE_TM
    tile_valid = (tile_start < pad_end[-1]).astype(jnp.int32)
    tile_e = jnp.minimum(jnp.searchsorted(pad_end, tile_start, side="right"), N_EXPERTS - 1).astype(jnp.int32)
    n_used = pad_end[-1] // MOE_TM
    tile_e = jnp.where(tile_valid > 0, tile_e, tile_e[jnp.maximum(n_used - 1, 0)])
    return top_w, buf_tok, dest, tile_e, tile_valid


def kernel(x_prompt, x_sample, cache_mem_k, cache_mem_v, state_ret, mem_prompt, g_mix, w_in, ln_v_g, w_s, b_s,
           ret_norm_g, g_mem, w_mk, w_mv, w_a, w_b, w_m, w_o, g_ffn, w_router, b_router, w1, b1, w2, b2, g_final):
    batch, seq, _ = x_prompt.shape
    n_dec = x_sample.shape[0]
    n_prompt_tok = batch * seq
    n_tok = n_prompt_tok + n_dec
    l = 0

    x = jnp.concatenate([x_prompt.reshape(n_prompt_tok, D_MODEL), x_sample.reshape(n_dec, D_MODEL)], axis=0)

    half = R_HEAD_DIM // 2
    inv = ROPE_BASE ** (-jnp.arange(half, dtype=F32) / half)

    def rope_tables(pos):
        ang = pos[:, None] * inv[None, :]
        cos, sin = jnp.cos(ang), jnp.sin(ang)
        return jnp.concatenate([cos, cos], axis=1), jnp.concatenate([-sin, sin], axis=1)

    cos_p, sin_p = rope_tables(jnp.arange(seq, dtype=F32))
    cos_s, sin_s = rope_tables(PAST_LEN + jnp.arange(1, dtype=F32))
    log_gamma = jnp.log(1.0 - 2.0 ** (-5.0 - jnp.arange(R_HEADS, dtype=F32)))
    idx = jnp.arange(CHUNK, dtype=F32)
    diff = idx[:, None] - idx[None, :]
    decay = jnp.where(diff[None] >= 0, jnp.exp(jnp.maximum(diff, 0.0)[None] * log_gamma[:, None, None]), 0.0)
    q_decay = jnp.exp((idx[:, None] + 1.0) * log_gamma[None, :])
    k_decay = jnp.exp((CHUNK - 1.0 - idx[:, None]) * log_gamma[None, :])
    chunk_decay = jnp.exp(CHUNK * log_gamma)
    gamma = jnp.exp(log_gamma)

    mk, mv = _memkv(mem_prompt.reshape(batch * N_MEM, D_MODEL), g_mem[l][None], w_mk[l].astype(BF16),
                    w_mv[l].astype(BF16))
    mk3 = mk.reshape(batch, N_MEM, M_WIDTH)
    mv3 = mv.reshape(batch, N_MEM, M_WIDTH)

    p = _inproj(x, g_mix[l][None], w_in[l].astype(BF16))

    a_p, b_p, m_p, vrows_p, state_p = _mixer_prompt(
        p, cos_p, sin_p, mk3, mv3, w_s[l], b_s[l].T, ln_v_g[l][None], ret_norm_g[l][None],
        decay, q_decay, k_decay, chunk_decay, batch, seq)

    ps = jnp.concatenate([p[n_prompt_tok:, :MAIN_W], p[n_prompt_tok:, MAIN_W + GATE_W:]], axis=1).astype(F32)
    w0 = jnp.repeat(w_s[l][:, 0, 0], A_GROUP_W)[None]
    b0 = jnp.repeat(b_s[l][:, 0], A_GROUP_W)[None]
    a_s, b_s_out, m_s, vrows_s, state_s = _mixer_sample(
        ps, cos_s, sin_s, cache_mem_k.reshape(1, n_dec, N_MEM, M_WIDTH), cache_mem_v.reshape(1, n_dec, N_MEM, M_WIDTH),
        state_ret[l], w0, b0, ln_v_g[l][None], ret_norm_g[l][None], gamma)

    a = jnp.concatenate([a_p, a_s.astype(BF16)], axis=0)
    b = jnp.concatenate([b_p, b_s_out.astype(BF16)], axis=0)
    m = jnp.concatenate([m_p, m_s.astype(BF16)], axis=0)

    lanes = 128
    wr = jnp.zeros((D_MODEL, lanes), F32).at[:, :N_EXPERTS].set(w_router[l])
    wr_hi = wr.astype(BF16)
    wr_lo = (wr - wr_hi.astype(F32)).astype(BF16)
    br = jnp.zeros((1, lanes), F32).at[0, :N_EXPERTS].set(b_router[l])
    y, h2, logits = _merge(a, b, m, p, x, w_a[l].astype(BF16), w_b[l].astype(BF16), w_m[l].astype(BF16),
                           w_o[l].astype(BF16), g_ffn[l][None], wr_hi, wr_lo, br)

    top_w, buf_tok, dest, tile_e, tile_valid = _route(logits[:, :N_EXPERTS], n_tok)
    xs = h2[buf_tok]
    ys = _moe(tile_e, tile_valid, xs, w1[l], b1[l][:, None, :], w2[l], b2[l][:, None, :])
    out = _combine(y, ys[dest], top_w, g_final[None])

    y_prompt = out[:n_prompt_tok].reshape(batch, seq, D_MODEL)
    y_sample = out[n_prompt_tok:].reshape(n_dec, 1, D_MODEL)
    return (y_prompt, y_sample,
            mk.reshape(1, batch, N_MEM, M_HEADS, M_HEAD_DIM), mv.reshape(1, batch, N_MEM, M_HEADS, M_HEAD_DIM),
            state_p, vrows_p, state_s, vrows_s.reshape(1, n_dec, 1, A_WIDTH))
```
